```python
import math
import jax, jax.numpy as jnp
from jax import lax
import numpy as np

D_MODEL = 1024
BATCH = 4
SEQ = 8192
DEPTH = 1

N_META = 16
D_MIX = D_MODEL
D_ATTN = D_MIX // 2
D_CONV = D_MIX - D_ATTN
ATTN_HEADS = 4
HEAD_DIM = D_ATTN // (2 * ATTN_HEADS)
V_HEAD_DIM = 2 * HEAD_DIM
ROPE_DIM = HEAD_DIM // 4
ROPE_THETA = 500000.0
CONV_WIDTH = 31
D_FF = 4 * D_MODEL
Q_BLOCK = 128
EPS = 1e-6
N_Q_COLS = ATTN_HEADS * 2 * HEAD_DIM
N_K_COLS = ATTN_HEADS * 2 * HEAD_DIM
N_V_COLS = ATTN_HEADS * V_HEAD_DIM
N_C_COLS = 2 * D_CONV
D_IN = N_Q_COLS + N_K_COLS + N_V_COLS + N_C_COLS

kernel_name = "hymba_diffattn_conformer_hybrid"


def rms_norm(x, g):
    xf = x.astype(jnp.float32)
    y = xf * lax.rsqrt(jnp.mean(xf * xf, axis=-1, keepdims=True) + EPS)
    return (y * g.astype(jnp.float32)).astype(x.dtype)


def layer_norm(x, g, b):
    xf = x.astype(jnp.float32)
    mu = jnp.mean(xf, axis=-1, keepdims=True)
    var = jnp.mean(jnp.square(xf - mu), axis=-1, keepdims=True)
    y = (xf - mu) * lax.rsqrt(var + EPS)
    return (y * g.astype(jnp.float32) + b.astype(jnp.float32)).astype(x.dtype)


def lambda_init(layer_idx):
    return 0.8 - 0.6 * math.exp(-0.3 * layer_idx)


def rope_tables(length):
    inv_freq = ROPE_THETA ** (-jnp.arange(0, ROPE_DIM, 2, dtype=jnp.float32) / ROPE_DIM)
    ang = jnp.arange(length, dtype=jnp.float32)[:, None] * inv_freq[None, :]
    return jnp.cos(ang), jnp.sin(ang)


def apply_partial_rope(t, cos, sin):
    c = cos[None, :, None, None, :].astype(t.dtype)
    s = sin[None, :, None, None, :].astype(t.dtype)
    half = ROPE_DIM // 2
    x1 = t[..., :half]
    x2 = t[..., half:ROPE_DIM]
    rot = jnp.concatenate([x1 * c - x2 * s, x2 * c + x1 * s], axis=-1)
    return jnp.concatenate([rot, t[..., ROPE_DIM:]], axis=-1)


def diff_attention(q, k, v, q_gain, k_gain, lam_q1, lam_k1, lam_q2, lam_k2, subln_gain, lam_init):
    B, L, _ = q.shape
    n_blocks = -(-L // Q_BLOCK)
    Lp = n_blocks * Q_BLOCK
    q = q.reshape(B, L, ATTN_HEADS, 2, HEAD_DIM)
    k = k.reshape(B, L, ATTN_HEADS, 2, HEAD_DIM)
    v = v.reshape(B, L, ATTN_HEADS, V_HEAD_DIM)
    q = rms_norm(q, q_gain)
    k = rms_norm(k, k_gain)
    pad = Lp - L
    q = jnp.pad(q, ((0, 0), (0, pad), (0, 0), (0, 0), (0, 0)))
    k = jnp.pad(k, ((0, 0), (0, pad), (0, 0), (0, 0), (0, 0)))
    v32 = jnp.pad(v, ((0, 0), (0, pad), (0, 0), (0, 0))).astype(jnp.float32)
    cos, sin = rope_tables(Lp)
    q = apply_partial_rope(q, cos, sin)
    k = apply_partial_rope(k, cos, sin)
    lam = (jnp.exp(jnp.sum(lam_q1.astype(jnp.float32) * lam_k1.astype(jnp.float32)))
           - jnp.exp(jnp.sum(lam_q2.astype(jnp.float32) * lam_k2.astype(jnp.float32)))
           + lam_init)
    scale = HEAD_DIM ** -0.5
    key_pos = jnp.arange(Lp)

    def block(i):
        start = i * Q_BLOCK
        qb = lax.dynamic_slice_in_dim(q, start, Q_BLOCK, axis=1)
        s = jnp.einsum('bqhcd,bkhcd->bhcqk', qb, k).astype(jnp.float32) * scale
        q_pos = start + jnp.arange(Q_BLOCK)
        mask = key_pos[None, :] <= q_pos[:, None]
        s = jnp.where(mask[None, None, None], s, -jnp.inf)
        p = jax.nn.softmax(s, axis=-1)
        a = p[:, :, 0] - lam * p[:, :, 1]
        return jnp.einsum('bhqk,bkhe->bqhe', a, v32)

    o = lax.map(block, jnp.arange(n_blocks))
    o = jnp.transpose(o, (1, 0, 2, 3, 4)).reshape(B, Lp, ATTN_HEADS, V_HEAD_DIM)[:, :L]
    o = o.astype(v.dtype)
    o = rms_norm(o, subln_gain) * (1.0 - lam_init)
    return o.reshape(B, L, ATTN_HEADS * V_HEAD_DIM)


def conformer_conv(u, conv_w, conv_b, ln_g, ln_b):
    a, g = jnp.split(u, 2, axis=-1)
    h = a * jax.nn.sigmoid(g)
    h = lax.conv_general_dilated(
        h, conv_w[:, None, :].astype(h.dtype), window_strides=(1,),
        padding=[(CONV_WIDTH - 1, 0)],
        dimension_numbers=('NWC', 'WIO', 'NWC'),
        feature_group_count=D_CONV) + conv_b.astype(h.dtype)
    h = layer_norm(h, ln_g, ln_b)
    return jax.nn.silu(h)


def setup_inputs(seed: int = 0) -> dict:
    key = jax.random.key(seed)
    ks = jax.random.split(key, 20)
    f = jnp.float32
    n = lambda k, shape, s: (jax.random.normal(k, shape, f) * s).astype(f)
    return {
        "x": n(ks[0], (BATCH, SEQ, D_MODEL), 1.0),
        "meta_tokens": n(ks[1], (N_META, D_MODEL), 1.0),
        "norm1_gain": 1.0 + n(ks[2], (DEPTH, D_MODEL), 0.02),
        "w_in": n(ks[3], (DEPTH, D_MODEL, D_IN), D_MODEL ** -0.5),
        "q_norm_gain": 1.0 + n(ks[4], (DEPTH, HEAD_DIM), 0.02),
        "k_norm_gain": 1.0 + n(ks[5], (DEPTH, HEAD_DIM), 0.02),
        "lambda_q1": n(ks[6], (DEPTH, HEAD_DIM), 0.1),
        "lambda_k1": n(ks[7], (DEPTH, HEAD_DIM), 0.1),
        "lambda_q2": n(ks[8], (DEPTH, HEAD_DIM), 0.1),
        "lambda_k2": n(ks[9], (DEPTH, HEAD_DIM), 0.1),
        "subln_gain": 1.0 + n(ks[10], (DEPTH, V_HEAD_DIM), 0.02),
        "conv_w": n(ks[11], (DEPTH, CONV_WIDTH, D_CONV), CONV_WIDTH ** -0.5),
        "conv_b": n(ks[12], (DEPTH, D_CONV), 0.02),
        "conv_ln_gain": 1.0 + n(ks[13], (DEPTH, D_CONV), 0.02),
        "conv_ln_bias": n(ks[14], (DEPTH, D_CONV), 0.02),
        "w_out": n(ks[15], (DEPTH, D_MIX, D_MODEL), D_MIX ** -0.5),
        "norm2_gain": 1.0 + n(ks[16], (DEPTH, D_MODEL), 0.02),
        "w_up": n(ks[17], (DEPTH, D_MODEL, D_FF), D_MODEL ** -0.5),
        "w_down": n(ks[18], (DEPTH, D_FF, D_MODEL), D_FF ** -0.5),
    }


def reference(x, meta_tokens, norm1_gain, w_in, q_norm_gain, k_norm_gain,
              lambda_q1, lambda_k1, lambda_q2, lambda_k2, subln_gain,
              conv_w, conv_b, conv_ln_gain, conv_ln_bias, w_out,
              norm2_gain, w_up, w_down):
    B = x.shape[0]
    meta = jnp.broadcast_to(meta_tokens[None].astype(x.dtype), (B, N_META, D_MODEL))
    h = jnp.concatenate([meta, x], axis=1)
    for l in range(DEPTH):
        hn = rms_norm(h, norm1_gain[l])
        proj = jnp.einsum('bld,de->ble', hn, w_in[l])
        q = proj[..., :N_Q_COLS]
        k = proj[..., N_Q_COLS:N_Q_COLS + N_K_COLS]
        v = proj[..., N_Q_COLS + N_K_COLS:N_Q_COLS + N_K_COLS + N_V_COLS]
        u = proj[..., N_Q_COLS + N_K_COLS + N_V_COLS:]
        attn_out = diff_attention(q, k, v, q_norm_gain[l], k_norm_gain[l],
                                  lambda_q1[l], lambda_k1[l], lambda_q2[l], lambda_k2[l],
                                  subln_gain[l], lambda_init(l))
        conv_out = conformer_conv(u, conv_w[l], conv_b[l],
                                  conv_ln_gain[l], conv_ln_bias[l])
        mixed = jnp.concatenate([attn_out, conv_out], axis=-1)
        h = h + jnp.einsum('ble,ed->bld', mixed, w_out[l])
        hn = rms_norm(h, norm2_gain[l])
        a = jax.nn.relu(jnp.einsum('bld,df->blf', hn, w_up[l]))
        h = h + jnp.einsum('blf,fd->bld', a * a, w_down[l])
    return h[:, N_META:]
```

```python
import functools
import math

import jax
import jax.numpy as jnp
from jax import lax
from jax.experimental import pallas as pl
from jax.experimental.pallas import tpu as pltpu

D_MODEL = 1024
N_META = 16
D_ATTN = D_MODEL // 2
D_CONV = D_MODEL - D_ATTN
ATTN_HEADS = 4
HEAD_DIM = D_ATTN // (2 * ATTN_HEADS)
V_HEAD_DIM = 2 * HEAD_DIM
ROPE_DIM = HEAD_DIM // 4
ROPE_THETA = 500000.0
CONV_WIDTH = 31
D_FF = 4 * D_MODEL
EPS = 1e-6
LAMBDA_INIT = 0.8 - 0.6 * math.exp(-0.3 * 0)
D_IN = 3 * D_ATTN + 2 * D_CONV

LANES = 128
CONV_HALO = 32
VMEM_LIMIT = 56 * 1024 * 1024

F32 = jnp.float32
BF16 = jnp.bfloat16


def _const_spec(shape):
    return pl.BlockSpec(shape, lambda *_: (0,) * len(shape), pipeline_mode=pl.Buffered(1))


def _head_norm_rope(t, gain, gmat, cos, sa, sb):
    msq = jnp.dot((t * t).astype(BF16), gmat, preferred_element_type=F32)
    tn = t * lax.rsqrt(msq + EPS) * gain
    cols = []
    for c in range(t.shape[1] // LANES):
        tc = tn[:, c * LANES:(c + 1) * LANES]
        cols.append(tc * cos + pltpu.roll(tc, LANES - ROPE_DIM // 2, 1) * sa + pltpu.roll(tc, ROPE_DIM // 2, 1) * sb)
    return jnp.concatenate(cols, axis=1)


def _proj_kernel(x_ref, g1_ref, win_ref, qg_ref, kg_ref, gmat_ref, cos_ref, sa_ref, sb_ref,
                 q_ref, k_ref, v_ref, hg_ref):
    x = x_ref[...]
    ms = jnp.mean(x * x, axis=-1, keepdims=True)
    xn = (x * lax.rsqrt(ms + EPS) * g1_ref[...]).astype(BF16)

    def proj(lo, width):
        return jnp.dot(xn, win_ref[:, lo:lo + width], preferred_element_type=F32)

    gmat = gmat_ref[...]
    cos, sa, sb = cos_ref[...], sa_ref[...], sb_ref[...]
    scale = HEAD_DIM ** -0.5
    q = _head_norm_rope(proj(0, D_ATTN), qg_ref[...] * scale, gmat, cos, sa, sb)
    q_ref[...] = q.astype(BF16)
    k = _head_norm_rope(proj(D_ATTN, D_ATTN), kg_ref[...], gmat, cos, sa, sb)
    k_ref[...] = k.astype(BF16)
    v_ref[...] = proj(2 * D_ATTN, D_ATTN).astype(BF16)
    ua = proj(3 * D_ATTN, D_CONV)
    ug = proj(3 * D_ATTN + D_CONV, D_CONV)
    hg_ref[...] = ua * jax.nn.sigmoid(ug)


def _proj_call(x2d, g1, win, qg, kg, gmat, cos, sa, sb, tm, name):
    rows = x2d.shape[0]
    n_tab = cos.shape[0] // tm
    row_spec = lambda w: pl.BlockSpec((tm, w), lambda i: (i, 0))
    tab_spec = pl.BlockSpec((tm, LANES), lambda i: (i % n_tab, 0))
    return pl.pallas_call(
        _proj_kernel,
        grid=(rows // tm,),
        in_specs=[row_spec(D_MODEL), _const_spec((1, D_MODEL)), _const_spec((D_MODEL, D_IN)),
                  _const_spec((1, D_ATTN)), _const_spec((1, D_ATTN)), _const_spec((D_ATTN, D_ATTN)),
                  tab_spec, tab_spec, tab_spec],
        out_specs=[row_spec(D_ATTN), row_spec(D_ATTN), row_spec(D_ATTN), row_spec(D_CONV)],
        out_shape=[jax.ShapeDtypeStruct((rows, D_ATTN), BF16), jax.ShapeDtypeStruct((rows, D_ATTN), BF16),
                   jax.ShapeDtypeStruct((rows, D_ATTN), BF16), jax.ShapeDtypeStruct((rows, D_CONV), F32)],
        compiler_params=pltpu.CompilerParams(dimension_semantics=("arbitrary",), vmem_limit_bytes=VMEM_LIMIT),
        name=name,
    )(x2d, g1, win, qg, kg, gmat, cos, sa, sb)


def _attn_kernel(q_ref, k_ref, v_ref, km_ref, vm_ref, lq1_ref, lk1_ref, lq2_ref, lk2_ref, sg_ref,
                 o_ref, q2_ref, m_ref, l_ref, acc_ref, *, tq, tk):
    qi = pl.program_id(2)
    q = q_ref[...]
    lane = lax.broadcasted_iota(jnp.int32, q.shape, 1)
    zero = jnp.zeros_like(q)
    q2_ref[0:tq, :] = jnp.where(lane < HEAD_DIM, q, zero)
    q2_ref[tq:2 * tq, :] = jnp.where(lane >= HEAD_DIM, q, zero)
    m_ref[...] = jnp.full(m_ref.shape, -jnp.inf, F32)
    l_ref[...] = jnp.zeros(l_ref.shape, F32)
    acc_ref[...] = jnp.zeros(acc_ref.shape, F32)

    def update(kb, vb, masked):
        s = lax.dot_general(q2_ref[...], kb, (((1,), (1,)), ((), ())), preferred_element_type=F32)
        if masked:
            row = lax.broadcasted_iota(jnp.int32, s.shape, 0)
            row = jnp.where(row >= tq, row - tq, row)
            col = lax.broadcasted_iota(jnp.int32, s.shape, 1)
            s = jnp.where(col <= row, s, -jnp.inf)
        m_prev = m_ref[...]
        m_new = jnp.maximum(m_prev, jnp.max(s, axis=1, keepdims=True))
        alpha = jnp.exp(m_prev - m_new)
        p = jnp.exp(s - m_new[:, :1])
        l_ref[...] = alpha * l_ref[...] + jnp.sum(p, axis=1, keepdims=True)
        acc_ref[...] = alpha * acc_ref[...] + jnp.dot(p.astype(BF16), vb, preferred_element_type=F32)
        m_ref[...] = m_new

    update(km_ref[...], vm_ref[...], False)

    def body(ki, carry):
        start = pl.multiple_of(ki * tk, tk)
        update(k_ref[pl.ds(start, tk), :], v_ref[pl.ds(start, tk), :], False)
        return carry

    lax.fori_loop(0, qi, body, 0)
    start = pl.multiple_of(qi * tk, tk)
    update(k_ref[pl.ds(start, tk), :], v_ref[pl.ds(start, tk), :], True)

    lam = (jnp.exp(jnp.sum(lq1_ref[...] * lk1_ref[...], axis=-1, keepdims=True))
           - jnp.exp(jnp.sum(lq2_ref[...] * lk2_ref[...], axis=-1, keepdims=True)) + LAMBDA_INIT)
    o = acc_ref[0:tq, :] / l_ref[0:tq, :] - lam * (acc_ref[tq:2 * tq, :] / l_ref[tq:2 * tq, :])
    o = o * lax.rsqrt(jnp.mean(o * o, axis=-1, keepdims=True) + EPS)
    o_ref[...] = (o * sg_ref[...] * (1.0 - LAMBDA_INIT)).astype(o_ref.dtype)


def _attn_call(q, k, v, km, vm, lq1, lk1, lq2, lk2, sg, batch, seq, tq, tk):
    assert tq == tk
    nq = seq // tq
    blk = lambda rows, imap: pl.BlockSpec((rows, LANES), imap)
    small = lambda w: pl.BlockSpec((1, w), lambda b, h, i: (0, 0))
    return pl.pallas_call(
        functools.partial(_attn_kernel, tq=tq, tk=tk),
        grid=(batch, ATTN_HEADS, nq),
        in_specs=[blk(tq, lambda b, h, i: (b * nq + i, h)),
                  blk(seq, lambda b, h, i: (b, h)),
                  blk(seq, lambda b, h, i: (b, h)),
                  blk(N_META, lambda b, h, i: (0, h)),
                  blk(N_META, lambda b, h, i: (0, h)),
                  small(HEAD_DIM), small(HEAD_DIM), small(HEAD_DIM), small(HEAD_DIM), small(V_HEAD_DIM)],
        out_specs=blk(tq, lambda b, h, i: (b * nq + i, h)),
        out_shape=jax.ShapeDtypeStruct((batch * seq, D_ATTN), BF16),
        scratch_shapes=[pltpu.VMEM((2 * tq, LANES), BF16), pltpu.VMEM((2 * tq, LANES), F32),
                        pltpu.VMEM((2 * tq, LANES), F32), pltpu.VMEM((2 * tq, LANES), F32)],
        compiler_params=pltpu.CompilerParams(dimension_semantics=("arbitrary",) * 3, vmem_limit_bytes=VMEM_LIMIT),
        name="diff_attn",
    )(q, k, v, km, vm, lq1, lk1, lq2, lk2, sg)


def _conv_kernel(hg_ref, halo_ref, hm_ref, w_ref, b_ref, lg_ref, lb_ref, o_ref, win_ref, sh_ref, *, tc, rc):
    j = pl.program_id(1)
    win_ref[CONV_HALO:, :] = hg_ref[...]

    @pl.when(j == 0)
    def _():
        win_ref[0:CONV_HALO - N_META, :] = jnp.zeros((CONV_HALO - N_META, D_CONV), F32)
        win_ref[CONV_HALO - N_META:CONV_HALO, :] = hm_ref[...]

    @pl.when(j > 0)
    def _():
        win_ref[0:CONV_HALO, :] = halo_ref[...]

    n_sh = tc + CONV_HALO - 8
    for r in range(1, 8):
        sh_ref[r - 1, :, :] = win_ref[r:r + n_sh, :]

    first = CONV_HALO - (CONV_WIDTH - 1)

    def chunk(c, carry):
        r0 = pl.multiple_of(c * rc, rc)
        acc = jnp.zeros((rc, D_CONV), F32) + b_ref[...]
        for tap in range(CONV_WIDTH):
            off = first + tap
            rows = pl.ds(r0 + (off // 8) * 8, rc)
            src = win_ref[rows, :] if off % 8 == 0 else sh_ref[off % 8 - 1, rows, :]
            acc = acc + src * w_ref[tap:tap + 1, :]
        mu = jnp.mean(acc, axis=-1, keepdims=True)
        cen = acc - mu
        var = jnp.mean(cen * cen, axis=-1, keepdims=True)
        y = cen * lax.rsqrt(var + EPS) * lg_ref[...] + lb_ref[...]
        o_ref[pl.ds(r0, rc), :] = (y * jax.nn.sigmoid(y)).astype(o_ref.dtype)
        return carry

    lax.fori_loop(0, tc // rc, chunk, 0)


def _conv_call(hg, hm, w, b, lg, lb, batch, seq, tc, rc):
    n = seq // tc
    per_halo = tc // CONV_HALO
    vec = _const_spec((1, D_CONV))
    return pl.pallas_call(
        functools.partial(_conv_kernel, tc=tc, rc=rc),
        grid=(batch, n),
        in_specs=[pl.BlockSpec((tc, D_CONV), lambda b_, j: (b_ * n + j, 0)),
                  pl.BlockSpec((CONV_HALO, D_CONV), lambda b_, j: (jnp.maximum((b_ * n + j) * per_halo - 1, 0), 0)),
                  _const_spec((N_META, D_CONV)), _const_spec((CONV_WIDTH, D_CONV)), vec, vec, vec],
        out_specs=pl.BlockSpec((tc, D_CONV), lambda b_, j: (b_ * n + j, 0)),
        out_shape=jax.ShapeDtypeStruct((batch * seq, D_CONV), BF16),
        scratch_shapes=[pltpu.VMEM((tc + CONV_HALO, D_CONV), F32),
                        pltpu.VMEM((7, tc + CONV_HALO - 8, D_CONV), F32)],
        compiler_params=pltpu.CompilerParams(dimension_semantics=("arbitrary",) * 2, vmem_limit_bytes=VMEM_LIMIT),
        name="conformer_conv",
    )(hg, hg, hm, w, b, lg, lb)


def _mlp_kernel(x_ref, a_ref, c_ref, wo_ref, g2_ref, wu_ref, wd_ref, o_ref, *, ff_chunk):
    mixed = (jnp.dot(a_ref[...], wo_ref[0:D_ATTN, :], preferred_element_type=F32)
             + jnp.dot(c_ref[...], wo_ref[D_ATTN:, :], preferred_element_type=F32))
    h = x_ref[...] + mixed
    ms = jnp.mean(h * h, axis=-1, keepdims=True)
    hn = (h * lax.rsqrt(ms + EPS) * g2_ref[...]).astype(BF16)

    def ff_part(c):
        up = jnp.dot(hn, wu_ref[:, c * ff_chunk:(c + 1) * ff_chunk], preferred_element_type=F32)
        up = jnp.maximum(up, 0.0)
        return jnp.dot((up * up).astype(BF16), wd_ref[c * ff_chunk:(c + 1) * ff_chunk, :],
                       preferred_element_type=F32)

    ff = ff_part(0)
    for c in range(1, D_FF // ff_chunk):
        ff = ff + ff_part(c)
    o_ref[...] = h + ff


def _mlp_call(x2d, attn, conv, wo, g2, wu, wd, tm, ff_chunk):
    rows = x2d.shape[0]
    row_spec = lambda w: pl.BlockSpec((tm, w), lambda i: (i, 0))
    return pl.pallas_call(
        functools.partial(_mlp_kernel, ff_chunk=ff_chunk),
        grid=(rows // tm,),
        in_specs=[row_spec(D_MODEL), row_spec(D_ATTN), row_spec(D_CONV),
                  _const_spec((D_MODEL, D_MODEL)), _const_spec((1, D_MODEL)),
                  _const_spec((D_MODEL, D_FF)), _const_spec((D_FF, D_MODEL))],
        out_specs=row_spec(D_MODEL),
        out_shape=jax.ShapeDtypeStruct((rows, D_MODEL), F32),
        compiler_params=pltpu.CompilerParams(dimension_semantics=("arbitrary",), vmem_limit_bytes=VMEM_LIMIT),
        name="outproj_mlp",
    )(x2d, attn, conv, wo, g2, wu, wd)


def _rope_tables(length):
    half = ROPE_DIM // 2
    inv_freq = ROPE_THETA ** (-jnp.arange(0, ROPE_DIM, 2, dtype=F32) / ROPE_DIM)
    ang = jnp.arange(length, dtype=F32)[:, None] * inv_freq[None, :]
    cos, sin = jnp.cos(ang), jnp.sin(ang)
    ones = jnp.ones((length, HEAD_DIM - ROPE_DIM), F32)
    zeros = jnp.zeros((length, HEAD_DIM - ROPE_DIM), F32)
    zh = jnp.zeros((length, half), F32)
    cos_g = jnp.concatenate([cos, cos, ones], axis=1)
    sa_g = jnp.concatenate([-sin, zh, zeros], axis=1)
    sb_g = jnp.concatenate([zh, sin, zeros], axis=1)
    rep = LANES // HEAD_DIM
    return jnp.tile(cos_g, (1, rep)), jnp.tile(sa_g, (1, rep)), jnp.tile(sb_g, (1, rep))


def kernel(x, meta_tokens, norm1_gain, w_in, q_norm_gain, k_norm_gain, lambda_q1, lambda_k1, lambda_q2,
           lambda_k2, subln_gain, conv_w, conv_b, conv_ln_gain, conv_ln_bias, w_out, norm2_gain, w_up, w_down):
    batch, seq, _ = x.shape
    assert norm1_gain.shape[0] == 1, "single-layer block"
    x2d = x.reshape(batch * seq, D_MODEL)

    g1 = norm1_gain[0].reshape(1, D_MODEL)
    g2 = norm2_gain[0].reshape(1, D_MODEL)
    win = w_in[0].astype(BF16)
    wo = w_out[0].astype(BF16)
    wu = w_up[0].astype(BF16)
    wd = w_down[0].astype(BF16)
    n_grp = D_ATTN // HEAD_DIM
    qg = jnp.tile(q_norm_gain[0], n_grp).reshape(1, D_ATTN)
    kg = jnp.tile(k_norm_gain[0], n_grp).reshape(1, D_ATTN)
    grp = jnp.arange(D_ATTN) // HEAD_DIM
    gmat = jnp.where(grp[:, None] == grp[None, :], 1.0 / HEAD_DIM, 0.0).astype(BF16)
    cos, sa, sb = _rope_tables(N_META + seq)

    tm = 512
    _, km, vm, hm = _proj_call(meta_tokens, g1, win, qg, kg, gmat, cos[:N_META], sa[:N_META], sb[:N_META],
                               N_META, "proj_meta")
    q, k, v, hg = _proj_call(x2d, g1, win, qg, kg, gmat, cos[N_META:], sa[N_META:], sb[N_META:], tm, "proj")

    row = lambda p, w: p[0].reshape(1, w)
    attn = _attn_call(q, k, v, km, vm, row(lambda_q1, HEAD_DIM), row(lambda_k1, HEAD_DIM),
                      row(lambda_q2, HEAD_DIM), row(lambda_k2, HEAD_DIM), row(subln_gain, V_HEAD_DIM),
                      batch, seq, 512, 512)
    conv = _conv_call(hg, hm, conv_w[0], row(conv_b, D_CONV), row(conv_ln_gain, D_CONV),
                      row(conv_ln_bias, D_CONV), batch, seq, 512, 32)
    out = _mlp_call(x2d, attn, conv, wo, g2, wu, wd, tm, 1024)
    return out.reshape(batch, seq, D_MODEL)
```

```python
import functools
import math

import jax
import jax.numpy as jnp
from jax import lax
from jax.experimental import pallas as pl
from jax.experimental.pallas import tpu as pltpu

D_MODEL = 1024
N_META = 16
D_ATTN = D_MODEL // 2
D_CONV = D_MODEL - D_ATTN
ATTN_HEADS = 4
HEAD_DIM = D_ATTN // (2 * ATTN_HEADS)
V_HEAD_DIM = 2 * HEAD_DIM
ROPE_DIM = HEAD_DIM // 4
ROPE_THETA = 500000.0
CONV_WIDTH = 31
D_FF = 4 * D_MODEL
EPS = 1e-6
LAMBDA_INIT = 0.8 - 0.6 * math.exp(-0.3 * 0)
D_IN = 3 * D_ATTN + 2 * D_CONV
LOG2_E = math.log2(math.e)

LANES = 128
CONV_HALO = 32
VMEM_LIMIT = 56 * 1024 * 1024

F32 = jnp.float32
BF16 = jnp.bfloat16


def _const_spec(shape):
    return pl.BlockSpec(shape, lambda *_: (0,) * len(shape), pipeline_mode=pl.Buffered(1))


def _head_norm_rope(t, gain, gmat, cos, sa, sb):
    msq = jnp.dot((t * t).astype(BF16), gmat, preferred_element_type=F32)
    tn = t * lax.rsqrt(msq + EPS) * gain
    cols = []
    for c in range(t.shape[1] // LANES):
        tc = tn[:, c * LANES:(c + 1) * LANES]
        cols.append(tc * cos + pltpu.roll(tc, LANES - ROPE_DIM // 2, 1) * sa + pltpu.roll(tc, ROPE_DIM // 2, 1) * sb)
    return jnp.concatenate(cols, axis=1)


def _proj_kernel(x_ref, g1_ref, win_ref, qg_ref, kg_ref, gmat_ref, cos_ref, sa_ref, sb_ref,
                 q_ref, k_ref, v_ref, hg_ref):
    x = x_ref[...]
    ms = jnp.mean(x * x, axis=-1, keepdims=True)
    xn = (x * lax.rsqrt(ms + EPS) * g1_ref[...]).astype(BF16)

    def proj(lo, width):
        return jnp.dot(xn, win_ref[:, lo:lo + width], preferred_element_type=F32)

    gmat = gmat_ref[...]
    cos, sa, sb = cos_ref[...], sa_ref[...], sb_ref[...]
    q = _head_norm_rope(proj(0, D_ATTN), qg_ref[...] * (HEAD_DIM ** -0.5 * LOG2_E), gmat, cos, sa, sb)
    q_ref[...] = q.astype(BF16)
    k = _head_norm_rope(proj(D_ATTN, D_ATTN), kg_ref[...], gmat, cos, sa, sb)
    k_ref[...] = k.astype(BF16)
    v_ref[...] = proj(2 * D_ATTN, D_ATTN).astype(BF16)
    ua = proj(3 * D_ATTN, D_CONV)
    ug = proj(3 * D_ATTN + D_CONV, D_CONV)
    hg_ref[...] = ua * jax.nn.sigmoid(ug)


def _proj_call(x2d, g1, win, qg, kg, gmat, cos, sa, sb, tm, name):
    rows = x2d.shape[0]
    n_tab = cos.shape[0] // tm
    row_spec = lambda w: pl.BlockSpec((tm, w), lambda i: (i, 0))
    tab_spec = pl.BlockSpec((tm, LANES), lambda i: (i % n_tab, 0))
    return pl.pallas_call(
        _proj_kernel,
        grid=(rows // tm,),
        in_specs=[row_spec(D_MODEL), _const_spec((1, D_MODEL)), _const_spec((D_MODEL, D_IN)),
                  _const_spec((1, D_ATTN)), _const_spec((1, D_ATTN)), _const_spec((D_ATTN, D_ATTN)),
                  tab_spec, tab_spec, tab_spec],
        out_specs=[row_spec(D_ATTN), row_spec(D_ATTN), row_spec(D_ATTN), row_spec(D_CONV)],
        out_shape=[jax.ShapeDtypeStruct((rows, D_ATTN), BF16), jax.ShapeDtypeStruct((rows, D_ATTN), BF16),
                   jax.ShapeDtypeStruct((rows, D_ATTN), BF16), jax.ShapeDtypeStruct((rows, D_CONV), F32)],
        compiler_params=pltpu.CompilerParams(dimension_semantics=("arbitrary",), vmem_limit_bytes=VMEM_LIMIT),
        name=name,
    )(x2d, g1, win, qg, kg, gmat, cos, sa, sb)


def _attn_kernel(q_ref, k_ref, v_ref, km_ref, vmt_ref, lq1_ref, lk1_ref, lq2_ref, lk2_ref, sg_ref,
                 o_ref, q2t_ref, vt_ref, m_ref, l_ref, alpha_ref, acc_ref, p_ref, s0_ref, s1_ref, mc0_ref, mc1_ref,
                 *, tq, tk, seq):
    qi = pl.program_id(2)
    nb = 2 * (qi + 1)

    @pl.when(qi == 0)
    def _():
        def xpose(c, carry):
            start = pl.multiple_of(c * tk, tk)
            vt_ref[c] = v_ref[pl.ds(start, tk), :].astype(F32).T.astype(BF16)
            return carry
        lax.fori_loop(0, seq // tk, xpose, 0)

    qt = q_ref[...].astype(F32).T
    feat = lax.broadcasted_iota(jnp.int32, qt.shape, 0)
    zero = jnp.zeros_like(qt)
    q2t_ref[:, 0:tq] = jnp.where(feat < HEAD_DIM, qt, zero).astype(BF16)
    q2t_ref[:, tq:2 * tq] = jnp.where(feat >= HEAD_DIM, qt, zero).astype(BF16)

    def scores(kb):
        return jnp.dot(kb, q2t_ref[...], preferred_element_type=F32)

    st = scores(km_ref[...])
    m0 = jnp.max(st, axis=0, keepdims=True)
    p0 = jnp.exp2(st - m0)
    m_ref[...] = m0
    l_ref[...] = jnp.sum(p0, axis=0, keepdims=True)
    acc_ref[...] = jnp.dot(vmt_ref[...], p0.astype(BF16), preferred_element_type=F32)

    def qk_stage(b, s_ref, mc_ref, masked):
        start = pl.multiple_of(b * tk, tk)
        st = scores(k_ref[pl.ds(start, tk), :])
        if masked:
            key = start + lax.broadcasted_iota(jnp.int32, st.shape, 0)
            col = lax.broadcasted_iota(jnp.int32, st.shape, 1)
            qpos = qi * tq + jnp.where(col >= tq, col - tq, col)
            st = jnp.where(key <= qpos, st, -jnp.inf)
        s_ref[...] = st
        mc_ref[...] = jnp.max(st, axis=0, keepdims=True)

    def softmax_stage(s_ref, mc_ref):
        st = s_ref[...]
        m_prev = m_ref[...]
        m_new = jnp.maximum(m_prev, mc_ref[...])
        alpha = jnp.exp2(m_prev - m_new)
        p = jnp.exp2(st - m_new)
        l_ref[...] = alpha * l_ref[...] + jnp.sum(p, axis=0, keepdims=True)
        m_ref[...] = m_new
        alpha_ref[...] = alpha
        p_ref[...] = p.astype(BF16)

    def pv_stage(b):
        acc_ref[...] = acc_ref[...] * alpha_ref[...] + jnp.dot(vt_ref[b], p_ref[...], preferred_element_type=F32)

    def pair(j, masked):
        pv_stage(2 * j - 2)
        qk_stage(2 * j, s0_ref, mc0_ref, masked)
        softmax_stage(s1_ref, mc1_ref)
        pv_stage(2 * j - 1)
        qk_stage(2 * j + 1, s1_ref, mc1_ref, masked)
        softmax_stage(s0_ref, mc0_ref)

    qk_stage(0, s0_ref, mc0_ref, True)
    qk_stage(1, s1_ref, mc1_ref, True)
    softmax_stage(s0_ref, mc0_ref)

    def body(j, carry):
        pair(j, False)
        return carry

    lax.fori_loop(1, qi, body, 0)

    @pl.when(qi > 0)
    def _():
        pair(qi, True)

    pv_stage(nb - 2)
    softmax_stage(s1_ref, mc1_ref)
    pv_stage(nb - 1)

    lam = (jnp.exp(jnp.sum(lq1_ref[...] * lk1_ref[...], axis=-1, keepdims=True))
           - jnp.exp(jnp.sum(lq2_ref[...] * lk2_ref[...], axis=-1, keepdims=True)) + LAMBDA_INIT)
    acc = acc_ref[...]
    l = l_ref[...]
    o = acc[:, 0:tq] / l[:, 0:tq] - lam * (acc[:, tq:2 * tq] / l[:, tq:2 * tq])
    o = o * lax.rsqrt(jnp.mean(o * o, axis=0, keepdims=True) + EPS)
    o = o * (sg_ref[...] * (1.0 - LAMBDA_INIT))
    o_ref[...] = o.T.astype(o_ref.dtype)


def _attn_call(q, k, v, km, vmt, lq1, lk1, lq2, lk2, sg, batch, seq, tq, tk):
    assert tq == 2 * tk
    nq = seq // tq
    blk = lambda rows, imap: pl.BlockSpec((rows, LANES), imap)
    small = lambda w: pl.BlockSpec((1, w), lambda b, h, i: (0, 0))
    return pl.pallas_call(
        functools.partial(_attn_kernel, tq=tq, tk=tk, seq=seq),
        grid=(batch, ATTN_HEADS, nq),
        in_specs=[blk(tq, lambda b, h, i: (b * nq + i, h)),
                  blk(seq, lambda b, h, i: (b, h)),
                  blk(seq, lambda b, h, i: (b, h)),
                  blk(N_META, lambda b, h, i: (0, h)),
                  pl.BlockSpec((V_HEAD_DIM, N_META), lambda b, h, i: (h, 0)),
                  small(HEAD_DIM), small(HEAD_DIM), small(HEAD_DIM), small(HEAD_DIM),
                  pl.BlockSpec((V_HEAD_DIM, 1), lambda b, h, i: (0, 0))],
        out_specs=blk(tq, lambda b, h, i: (b * nq + i, h)),
        out_shape=jax.ShapeDtypeStruct((batch * seq, D_ATTN), BF16),
        scratch_shapes=[pltpu.VMEM((V_HEAD_DIM, 2 * tq), BF16),
                        pltpu.VMEM((seq // tk, V_HEAD_DIM, tk), BF16),
                        pltpu.VMEM((1, 2 * tq), F32), pltpu.VMEM((1, 2 * tq), F32), pltpu.VMEM((1, 2 * tq), F32),
                        pltpu.VMEM((V_HEAD_DIM, 2 * tq), F32),
                        pltpu.VMEM((tk, 2 * tq), BF16),
                        pltpu.VMEM((tk, 2 * tq), F32), pltpu.VMEM((tk, 2 * tq), F32),
                        pltpu.VMEM((1, 2 * tq), F32), pltpu.VMEM((1, 2 * tq), F32)],
        compiler_params=pltpu.CompilerParams(dimension_semantics=("arbitrary",) * 3, vmem_limit_bytes=VMEM_LIMIT),
        name="diff_attn",
    )(q, k, v, km, vmt, lq1, lk1, lq2, lk2, sg)


def _conv_kernel(hg_ref, halo_ref, hm_ref, w_ref, b_ref, lg_ref, lb_ref, o_ref, win_ref, sh_ref, *, tc, rc):
    j = pl.program_id(1)
    win_ref[CONV_HALO:, :] = hg_ref[...]

    @pl.when(j == 0)
    def _():
        win_ref[0:CONV_HALO - N_META, :] = jnp.zeros((CONV_HALO - N_META, D_CONV), F32)
        win_ref[CONV_HALO - N_META:CONV_HALO, :] = hm_ref[...]

    @pl.when(j > 0)
    def _():
        win_ref[0:CONV_HALO, :] = halo_ref[...]

    n_sh = tc + CONV_HALO - 8
    for r in range(1, 8):
        sh_ref[r - 1, :, :] = win_ref[r:r + n_sh, :]

    first = CONV_HALO - (CONV_WIDTH - 1)

    def chunk(c, carry):
        r0 = pl.multiple_of(c * rc, rc)
        acc = jnp.zeros((rc, D_CONV), F32) + b_ref[...]
        for tap in range(CONV_WIDTH):
            off = first + tap
            rows = pl.ds(r0 + (off // 8) * 8, rc)
            src = win_ref[rows, :] if off % 8 == 0 else sh_ref[off % 8 - 1, rows, :]
            acc = acc + src * w_ref[tap:tap + 1, :]
        mu = jnp.mean(acc, axis=-1, keepdims=True)
        cen = acc - mu
        var = jnp.mean(cen * cen, axis=-1, keepdims=True)
        y = cen * lax.rsqrt(var + EPS) * lg_ref[...] + lb_ref[...]
        o_ref[pl.ds(r0, rc), :] = (y * jax.nn.sigmoid(y)).astype(o_ref.dtype)
        return carry

    lax.fori_loop(0, tc // rc, chunk, 0)


def _conv_call(hg, hm, w, b, lg, lb, batch, seq, tc, rc):
    n = seq // tc
    per_halo = tc // CONV_HALO
    vec = _const_spec((1, D_CONV))
    return pl.pallas_call(
        functools.partial(_conv_kernel, tc=tc, rc=rc),
        grid=(batch, n),
        in_specs=[pl.BlockSpec((tc, D_CONV), lambda b_, j: (b_ * n + j, 0)),
                  pl.BlockSpec((CONV_HALO, D_CONV), lambda b_, j: (jnp.maximum((b_ * n + j) * per_halo - 1, 0), 0)),
                  _const_spec((N_META, D_CONV)), _const_spec((CONV_WIDTH, D_CONV)), vec, vec, vec],
        out_specs=pl.BlockSpec((tc, D_CONV), lambda b_, j: (b_ * n + j, 0)),
        out_shape=jax.ShapeDtypeStruct((batch * seq, D_CONV), BF16),
        scratch_shapes=[pltpu.VMEM((tc + CONV_HALO, D_CONV), F32),
                        pltpu.VMEM((7, tc + CONV_HALO - 8, D_CONV), F32)],
        compiler_params=pltpu.CompilerParams(dimension_semantics=("arbitrary",) * 2, vmem_limit_bytes=VMEM_LIMIT),
        name="conformer_conv",
    )(hg, hg, hm, w, b, lg, lb)


def _mlp_kernel(x_ref, a_ref, c_ref, wo_ref, g2_ref, wu_ref, wd_ref, o_ref, *, ff_chunk):
    mixed = (jnp.dot(a_ref[...], wo_ref[0:D_ATTN, :], preferred_element_type=F32)
             + jnp.dot(c_ref[...], wo_ref[D_ATTN:, :], preferred_element_type=F32))
    h = x_ref[...] + mixed
    ms = jnp.mean(h * h, axis=-1, keepdims=True)
    hn = (h * lax.rsqrt(ms + EPS) * g2_ref[...]).astype(BF16)

    def ff_part(c):
        up = jnp.dot(hn, wu_ref[:, c * ff_chunk:(c + 1) * ff_chunk], preferred_element_type=F32)
        up = jnp.maximum(up, 0.0)
        return jnp.dot((up * up).astype(BF16), wd_ref[c * ff_chunk:(c + 1) * ff_chunk, :],
                       preferred_element_type=F32)

    ff = ff_part(0)
    for c in range(1, D_FF // ff_chunk):
        ff = ff + ff_part(c)
    o_ref[...] = h + ff


def _mlp_call(x2d, attn, conv, wo, g2, wu, wd, tm, ff_chunk):
    rows = x2d.shape[0]
    row_spec = lambda w: pl.BlockSpec((tm, w), lambda i: (i, 0))
    return pl.pallas_call(
        functools.partial(_mlp_kernel, ff_chunk=ff_chunk),
        grid=(rows // tm,),
        in_specs=[row_spec(D_MODEL), row_spec(D_ATTN), row_spec(D_CONV),
                  _const_spec((D_MODEL, D_MODEL)), _const_spec((1, D_MODEL)),
                  _const_spec((D_MODEL, D_FF)), _const_spec((D_FF, D_MODEL))],
        out_specs=row_spec(D_MODEL),
        out_shape=jax.ShapeDtypeStruct((rows, D_MODEL), F32),
        compiler_params=pltpu.CompilerParams(dimension_semantics=("arbitrary",), vmem_limit_bytes=VMEM_LIMIT),
        name="outproj_mlp",
    )(x2d, attn, conv, wo, g2, wu, wd)


def _rope_tables(length):
    half = ROPE_DIM // 2
    inv_freq = ROPE_THETA ** (-jnp.arange(0, ROPE_DIM, 2, dtype=F32) / ROPE_DIM)
    ang = jnp.arange(length, dtype=F32)[:, None] * inv_freq[None, :]
    cos, sin = jnp.cos(ang), jnp.sin(ang)
    ones = jnp.ones((length, HEAD_DIM - ROPE_DIM), F32)
    zeros = jnp.zeros((length, HEAD_DIM - ROPE_DIM), F32)
    zh = jnp.zeros((length, half), F32)
    cos_g = jnp.concatenate([cos, cos, ones], axis=1)
    sa_g = jnp.concatenate([-sin, zh, zeros], axis=1)
    sb_g = jnp.concatenate([zh, sin, zeros], axis=1)
    rep = LANES // HEAD_DIM
    return jnp.tile(cos_g, (1, rep)), jnp.tile(sa_g, (1, rep)), jnp.tile(sb_g, (1, rep))


def kernel(x, meta_tokens, norm1_gain, w_in, q_norm_gain, k_norm_gain, lambda_q1, lambda_k1, lambda_q2,
           lambda_k2, subln_gain, conv_w, conv_b, conv_ln_gain, conv_ln_bias, w_out, norm2_gain, w_up, w_down):
    batch, seq, _ = x.shape
    assert norm1_gain.shape[0] == 1, "single-layer block"
    x2d = x.reshape(batch * seq, D_MODEL)

    g1 = norm1_gain[0].reshape(1, D_MODEL)
    g2 = norm2_gain[0].reshape(1, D_MODEL)
    win = w_in[0].astype(BF16)
    wo = w_out[0].astype(BF16)
    wu = w_up[0].astype(BF16)
    wd = w_down[0].astype(BF16)
    n_grp = D_ATTN // HEAD_DIM
    qg = jnp.tile(q_norm_gain[0], n_grp).reshape(1, D_ATTN)
    kg = jnp.tile(k_norm_gain[0], n_grp).reshape(1, D_ATTN)
    grp = jnp.arange(D_ATTN) // HEAD_DIM
    gmat = jnp.where(grp[:, None] == grp[None, :], 1.0 / HEAD_DIM, 0.0).astype(BF16)
    cos, sa, sb = _rope_tables(N_META + seq)

    tm = 512
    _, km, vm, hm = _proj_call(meta_tokens, g1, win, qg, kg, gmat, cos[:N_META], sa[:N_META], sb[:N_META],
                               N_META, "proj_meta")
    q, k, v, hg = _proj_call(x2d, g1, win, qg, kg, gmat, cos[N_META:], sa[N_META:], sb[N_META:], tm, "proj")

    row = lambda p, w: p[0].reshape(1, w)
    attn = _attn_call(q, k, v, km, vm.T, row(lambda_q1, HEAD_DIM), row(lambda_k1, HEAD_DIM),
                      row(lambda_q2, HEAD_DIM), row(lambda_k2, HEAD_DIM), subln_gain[0].reshape(V_HEAD_DIM, 1),
                      batch, seq, 512, 256)
    conv = _conv_call(hg, hm, conv_w[0], row(conv_b, D_CONV), row(conv_ln_gain, D_CONV),
                      row(conv_ln_bias, D_CONV), batch, seq, 512, 32)
    out = _mlp_call(x2d, attn, conv, wo, g2, wu, wd, tm, 1024)
    return out.reshape(batch, seq, D_MODEL)
```

```python
import functools
import math

import jax
import jax.numpy as jnp
from jax import lax
from jax.experimental import pallas as pl
from jax.experimental.pallas import tpu as pltpu

D_MODEL = 1024
N_META = 16
D_ATTN = D_MODEL // 2
D_CONV = D_MODEL - D_ATTN
ATTN_HEADS = 4
HEAD_DIM = D_ATTN // (2 * ATTN_HEADS)
V_HEAD_DIM = 2 * HEAD_DIM
ROPE_DIM = HEAD_DIM // 4
ROPE_THETA = 500000.0
CONV_WIDTH = 31
D_FF = 4 * D_MODEL
EPS = 1e-6
LAMBDA_INIT = 0.8 - 0.6 * math.exp(-0.3 * 0)
D_IN = 3 * D_ATTN + 2 * D_CONV
LOG2_E = math.log2(math.e)

LANES = 128
CONV_HALO = 32
VMEM_LIMIT = 56 * 1024 * 1024

F32 = jnp.float32
BF16 = jnp.bfloat16


def _const_spec(shape):
    return pl.BlockSpec(shape, lambda *_: (0,) * len(shape), pipeline_mode=pl.Buffered(1))


def _head_norm_rope(t, gain, gmat, cos, sa, sb):
    msq = jnp.dot((t * t).astype(BF16), gmat, preferred_element_type=F32)
    tn = t * lax.rsqrt(msq + EPS) * gain
    cols = []
    for c in range(t.shape[1] // LANES):
        tc = tn[:, c * LANES:(c + 1) * LANES]
        cols.append(tc * cos + pltpu.roll(tc, LANES - ROPE_DIM // 2, 1) * sa + pltpu.roll(tc, ROPE_DIM // 2, 1) * sb)
    return jnp.concatenate(cols, axis=1)


def _normed_input(x_ref, g1_ref):
    x = x_ref[...]
    ms = jnp.mean(x * x, axis=-1, keepdims=True)
    return (x * lax.rsqrt(ms + EPS) * g1_ref[...]).astype(BF16)


def _proj_meta_kernel(x_ref, g1_ref, win_ref, kg_ref, gmat_ref, cos_ref, sa_ref, sb_ref, k_ref, v_ref, hg_ref):
    xn = _normed_input(x_ref, g1_ref)
    proj = lambda lo, width: jnp.dot(xn, win_ref[:, lo:lo + width], preferred_element_type=F32)
    k = _head_norm_rope(proj(D_ATTN, D_ATTN), kg_ref[...], gmat_ref[...], cos_ref[...], sa_ref[...], sb_ref[...])
    k_ref[...] = k.astype(BF16)
    v_ref[...] = proj(2 * D_ATTN, D_ATTN).astype(BF16)
    hg_ref[...] = proj(3 * D_ATTN, D_CONV) * jax.nn.sigmoid(proj(3 * D_ATTN + D_CONV, D_CONV))


def _proj_meta_call(meta, g1, win, kg, gmat, cos, sa, sb):
    full = lambda a: pl.BlockSpec(a.shape, lambda: (0,) * a.ndim)
    args = (meta, g1, win, kg, gmat, cos, sa, sb)
    return pl.pallas_call(
        _proj_meta_kernel,
        in_specs=[full(a) for a in args],
        out_specs=[pl.BlockSpec((N_META, w), lambda: (0, 0)) for w in (D_ATTN, D_ATTN, D_CONV)],
        out_shape=[jax.ShapeDtypeStruct((N_META, D_ATTN), BF16), jax.ShapeDtypeStruct((N_META, D_ATTN), BF16),
                   jax.ShapeDtypeStruct((N_META, D_CONV), F32)],
        compiler_params=pltpu.CompilerParams(vmem_limit_bytes=VMEM_LIMIT),
        name="proj_meta",
    )(*args)


def _proj_kernel(x_ref, g1_ref, win_ref, qg_ref, kg_ref, gmat_ref, cos_ref, sa_ref, sb_ref,
                 hm_ref, cw_ref, cb_ref, lg_ref, lb_ref,
                 q_ref, k_ref, v_ref, c_ref, hwin_ref, sh_ref, *, tm, tiles_per_batch, rc):
    j = pl.program_id(0) % tiles_per_batch
    xn = _normed_input(x_ref, g1_ref)
    proj = lambda lo, width: jnp.dot(xn, win_ref[:, lo:lo + width], preferred_element_type=F32)

    hg = proj(3 * D_ATTN, D_CONV) * jax.nn.sigmoid(proj(3 * D_ATTN + D_CONV, D_CONV))

    @pl.when(j == 0)
    def _():
        hwin_ref[0:CONV_HALO - N_META, :] = jnp.zeros((CONV_HALO - N_META, D_CONV), F32)
        hwin_ref[CONV_HALO - N_META:CONV_HALO, :] = hm_ref[...]

    @pl.when(j > 0)
    def _():
        hwin_ref[0:CONV_HALO, :] = hwin_ref[tm:tm + CONV_HALO, :]

    hwin_ref[CONV_HALO:, :] = hg
    n_sh = tm + CONV_HALO - 8
    for r in range(1, 8):
        sh_ref[r - 1, :, :] = hwin_ref[r:r + n_sh, :]

    first = CONV_HALO - (CONV_WIDTH - 1)
    for c in range(tm // rc):
        r0 = c * rc
        acc = jnp.zeros((rc, D_CONV), F32) + cb_ref[...]
        for tap in range(CONV_WIDTH):
            off = first + tap
            lo = r0 + (off // 8) * 8
            src = hwin_ref[lo:lo + rc, :] if off % 8 == 0 else sh_ref[off % 8 - 1, lo:lo + rc, :]
            acc = acc + src * cw_ref[tap:tap + 1, :]
        mu = jnp.mean(acc, axis=-1, keepdims=True)
        cen = acc - mu
        var = jnp.mean(cen * cen, axis=-1, keepdims=True)
        y = cen * lax.rsqrt(var + EPS) * lg_ref[...] + lb_ref[...]
        c_ref[r0:r0 + rc, :] = (y * jax.nn.sigmoid(y)).astype(c_ref.dtype)

    gmat = gmat_ref[...]
    cos, sa, sb = cos_ref[...], sa_ref[...], sb_ref[...]
    q = _head_norm_rope(proj(0, D_ATTN), qg_ref[...] * (HEAD_DIM ** -0.5 * LOG2_E), gmat, cos, sa, sb)
    q_ref[...] = q.astype(BF16)
    k = _head_norm_rope(proj(D_ATTN, D_ATTN), kg_ref[...], gmat, cos, sa, sb)
    k_ref[...] = k.astype(BF16)
    v_ref[...] = proj(2 * D_ATTN, D_ATTN).astype(BF16)


def _proj_call(x2d, g1, win, qg, kg, gmat, cos, sa, sb, hm, cw, cb, lg, lb, tm, tiles_per_batch, rc):
    rows = x2d.shape[0]
    row_spec = lambda w: pl.BlockSpec((tm, w), lambda i: (i, 0))
    tab_spec = pl.BlockSpec((tm, LANES), lambda i: (i % tiles_per_batch, 0))
    vec = lambda w: _const_spec((1, w))
    return pl.pallas_call(
        functools.partial(_proj_kernel, tm=tm, tiles_per_batch=tiles_per_batch, rc=rc),
        grid=(rows // tm,),
        in_specs=[row_spec(D_MODEL), vec(D_MODEL), _const_spec((D_MODEL, D_IN)),
                  vec(D_ATTN), vec(D_ATTN), _const_spec((D_ATTN, D_ATTN)),
                  tab_spec, tab_spec, tab_spec,
                  _const_spec((N_META, D_CONV)), _const_spec((CONV_WIDTH, D_CONV)), vec(D_CONV), vec(D_CONV), vec(D_CONV)],
        out_specs=[row_spec(D_ATTN), row_spec(D_ATTN), row_spec(D_ATTN), row_spec(D_CONV)],
        out_shape=[jax.ShapeDtypeStruct((rows, D_ATTN), BF16), jax.ShapeDtypeStruct((rows, D_ATTN), BF16),
                   jax.ShapeDtypeStruct((rows, D_ATTN), BF16), jax.ShapeDtypeStruct((rows, D_CONV), BF16)],
        scratch_shapes=[pltpu.VMEM((tm + CONV_HALO, D_CONV), F32),
                        pltpu.VMEM((7, tm + CONV_HALO - 8, D_CONV), F32)],
        compiler_params=pltpu.CompilerParams(dimension_semantics=("arbitrary",), vmem_limit_bytes=VMEM_LIMIT),
        name="proj_conv",
    )(x2d, g1, win, qg, kg, gmat, cos, sa, sb, hm, cw, cb, lg, lb)


KX = 2 * LANES
VX = V_HEAD_DIM + 16
OVERFLOW_GUARD = 2.0 ** 100


def _split3(r):
    hi = r.astype(BF16).astype(F32)
    mid = (r - hi).astype(BF16).astype(F32)
    lo = (r - hi - mid).astype(BF16).astype(F32)
    return hi, mid, lo


def _attn_kernel(q_ref, k_ref, v_ref, km_ref, vmt_ref, lq1_ref, lk1_ref, lq2_ref, lk2_ref, sg_ref,
                 o_ref, kx_ref, vtx_ref, q2x_ref, acc_ref, p_ref, *, t, seq):
    qi = pl.program_id(2)

    @pl.when(qi == 0)
    def _():
        ones_col = (lax.broadcasted_iota(jnp.int32, (t, LANES), 1) < 3).astype(BF16)
        tail = (lax.broadcasted_iota(jnp.int32, (VX - V_HEAD_DIM, t), 0) == 0).astype(BF16)

        def ext(c, carry):
            rows = pl.ds(pl.multiple_of(c * t, t), t)
            kx_ref[rows, 0:LANES] = k_ref[rows, :]
            kx_ref[rows, LANES:KX] = ones_col
            vtx_ref[c, 0:V_HEAD_DIM, :] = v_ref[rows, :].astype(F32).T.astype(BF16)
            vtx_ref[c, V_HEAD_DIM:VX, :] = tail
            return carry
        lax.fori_loop(0, seq // t, ext, 0)

    qt = q_ref[...].astype(F32).T
    feat = lax.broadcasted_iota(jnp.int32, qt.shape, 0)
    zero = jnp.zeros_like(qt)
    q2x_ref[0:LANES, 0:t] = jnp.where(feat < HEAD_DIM, qt, zero).astype(BF16)
    q2x_ref[0:LANES, t:2 * t] = jnp.where(feat >= HEAD_DIM, qt, zero).astype(BF16)
    q2x_ref[LANES:KX, :] = jnp.zeros((KX - LANES, 2 * t), BF16)

    ones16 = (lax.broadcasted_iota(jnp.int32, (N_META, LANES), 1) < 3).astype(BF16)
    kmx = jnp.concatenate([km_ref[...], ones16], axis=1)
    vtail = (lax.broadcasted_iota(jnp.int32, (VX - V_HEAD_DIM, N_META), 0) == 0).astype(BF16)
    vmtx = jnp.concatenate([vmt_ref[...], vtail], axis=0)

    def causal(st, g):
        key = g * t + lax.broadcasted_iota(jnp.int32, st.shape, 0)
        col = lax.broadcasted_iota(jnp.int32, st.shape, 1)
        qpos = qi * t + jnp.where(col >= t, col - t, col)
        return jnp.where(key <= qpos, st, -jnp.inf)

    def scores(g):
        rows = pl.ds(pl.multiple_of(g * t, t), t)
        return jnp.dot(kx_ref[rows, :], q2x_ref[...], preferred_element_type=F32)

    def qk_exp(g, masked):
        st = scores(g)
        if masked:
            st = causal(st, g)
        p_ref[g & 1] = jnp.exp2(st).astype(BF16)

    def pv(g):
        acc_ref[...] += jnp.dot(vtx_ref[g], p_ref[g & 1], preferred_element_type=F32)

    def run(ref_max):
        hi, mid, lo = _split3(ref_max)
        pad = jnp.zeros((16 - 3, 2 * t), F32)
        q2x_ref[LANES:LANES + 16, :] = (-jnp.concatenate([hi, mid, lo, pad], axis=0)).astype(BF16)
        pm = jnp.exp2(jnp.dot(kmx, q2x_ref[...], preferred_element_type=F32))
        acc_ref[...] = jnp.dot(vmtx, pm.astype(BF16), preferred_element_type=F32)
        qk_exp(0, True)

        def body(g, carry):
            pv(g - 1)
            qk_exp(g, False)
            return carry
        lax.fori_loop(1, qi, body, 0)

        @pl.when(qi > 0)
        def _():
            pv(qi - 1)
            qk_exp(qi, True)
        pv(qi)

    run(jnp.max(jnp.dot(kmx, q2x_ref[...], preferred_element_type=F32), axis=0, keepdims=True))

    @pl.when(jnp.logical_not(jnp.sum(acc_ref[V_HEAD_DIM:V_HEAD_DIM + 1, :]) < OVERFLOW_GUARD))
    def _():
        q2x_ref[LANES:LANES + 16, :] = jnp.zeros((16, 2 * t), BF16)
        m0 = jnp.max(jnp.dot(kmx, q2x_ref[...], preferred_element_type=F32), axis=0, keepdims=True)

        def mx(g, m):
            return jnp.maximum(m, jnp.max(scores(g), axis=0, keepdims=True))
        m1 = lax.fori_loop(0, qi, mx, m0)
        run(jnp.maximum(m1, jnp.max(causal(scores(qi), qi), axis=0, keepdims=True)))

    lam = (jnp.exp(jnp.sum(lq1_ref[...] * lk1_ref[...], axis=-1, keepdims=True))
           - jnp.exp(jnp.sum(lq2_ref[...] * lk2_ref[...], axis=-1, keepdims=True)) + LAMBDA_INIT)
    acc = acc_ref[0:V_HEAD_DIM, :]
    l = acc_ref[V_HEAD_DIM:V_HEAD_DIM + 1, :]
    o = acc[:, 0:t] / l[:, 0:t] - lam * (acc[:, t:2 * t] / l[:, t:2 * t])
    o = o * lax.rsqrt(jnp.mean(o * o, axis=0, keepdims=True) + EPS)
    o = o * (sg_ref[...] * (1.0 - LAMBDA_INIT))
    o_ref[...] = o.T.astype(o_ref.dtype)


def _attn_call(q, k, v, km, vmt, lq1, lk1, lq2, lk2, sg, batch, seq, t):
    nq = seq // t
    blk = lambda rows, imap: pl.BlockSpec((rows, LANES), imap)
    small = lambda w: pl.BlockSpec((1, w), lambda b, h, i: (0, 0))
    return pl.pallas_call(
        functools.partial(_attn_kernel, t=t, seq=seq),
        grid=(batch, ATTN_HEADS, nq),
        in_specs=[blk(t, lambda b, h, i: (b * nq + i, h)),
                  blk(seq, lambda b, h, i: (b, h)),
                  blk(seq, lambda b, h, i: (b, h)),
                  blk(N_META, lambda b, h, i: (0, h)),
                  pl.BlockSpec((V_HEAD_DIM, N_META), lambda b, h, i: (h, 0)),
                  small(HEAD_DIM), small(HEAD_DIM), small(HEAD_DIM), small(HEAD_DIM),
                  pl.BlockSpec((V_HEAD_DIM, 1), lambda b, h, i: (0, 0))],
        out_specs=blk(t, lambda b, h, i: (b * nq + i, h)),
        out_shape=jax.ShapeDtypeStruct((batch * seq, D_ATTN), BF16),
        scratch_shapes=[pltpu.VMEM((seq, KX), BF16),
                        pltpu.VMEM((seq // t, VX, t), BF16),
                        pltpu.VMEM((KX, 2 * t), BF16),
                        pltpu.VMEM((VX, 2 * t), F32),
                        pltpu.VMEM((2, t, 2 * t), BF16)],
        compiler_params=pltpu.CompilerParams(dimension_semantics=("arbitrary",) * 3, vmem_limit_bytes=VMEM_LIMIT),
        name="diff_attn",
    )(q, k, v, km, vmt, lq1, lk1, lq2, lk2, sg)


def _mlp_kernel(x_ref, a_ref, c_ref, wo_ref, g2_ref, wu_ref, wd_ref, o_ref, *, ff_chunk):
    mixed = (jnp.dot(a_ref[...], wo_ref[0:D_ATTN, :], preferred_element_type=F32)
             + jnp.dot(c_ref[...], wo_ref[D_ATTN:, :], preferred_element_type=F32))
    h = x_ref[...] + mixed
    ms = jnp.mean(h * h, axis=-1, keepdims=True)
    hn = (h * lax.rsqrt(ms + EPS) * g2_ref[...]).astype(BF16)

    def ff_part(c):
        up = jnp.dot(hn, wu_ref[:, c * ff_chunk:(c + 1) * ff_chunk], preferred_element_type=F32)
        up = jnp.maximum(up, 0.0)
        return jnp.dot((up * up).astype(BF16), wd_ref[c * ff_chunk:(c + 1) * ff_chunk, :],
                       preferred_element_type=F32)

    ff = ff_part(0)
    for c in range(1, D_FF // ff_chunk):
        ff = ff + ff_part(c)
    o_ref[...] = h + ff


def _mlp_call(x2d, attn, conv, wo, g2, wu, wd, tm, ff_chunk):
    rows = x2d.shape[0]
    row_spec = lambda w: pl.BlockSpec((tm, w), lambda i: (i, 0))
    return pl.pallas_call(
        functools.partial(_mlp_kernel, ff_chunk=ff_chunk),
        grid=(rows // tm,),
        in_specs=[row_spec(D_MODEL), row_spec(D_ATTN), row_spec(D_CONV),
                  _const_spec((D_MODEL, D_MODEL)), _const_spec((1, D_MODEL)),
                  _const_spec((D_MODEL, D_FF)), _const_spec((D_FF, D_MODEL))],
        out_specs=row_spec(D_MODEL),
        out_shape=jax.ShapeDtypeStruct((rows, D_MODEL), F32),
        compiler_params=pltpu.CompilerParams(dimension_semantics=("arbitrary",), vmem_limit_bytes=VMEM_LIMIT),
        name="outproj_mlp",
    )(x2d, attn, conv, wo, g2, wu, wd)


def _rope_tables(length):
    half = ROPE_DIM // 2
    inv_freq = ROPE_THETA ** (-jnp.arange(0, ROPE_DIM, 2, dtype=F32) / ROPE_DIM)
    ang = jnp.arange(length, dtype=F32)[:, None] * inv_freq[None, :]
    cos, sin = jnp.cos(ang), jnp.sin(ang)
    ones = jnp.ones((length, HEAD_DIM - ROPE_DIM), F32)
    zeros = jnp.zeros((length, HEAD_DIM - ROPE_DIM), F32)
    zh = jnp.zeros((length, half), F32)
    cos_g = jnp.concatenate([cos, cos, ones], axis=1)
    sa_g = jnp.concatenate([-sin, zh, zeros], axis=1)
    sb_g = jnp.concatenate([zh, sin, zeros], axis=1)
    rep = LANES // HEAD_DIM
    return jnp.tile(cos_g, (1, rep)), jnp.tile(sa_g, (1, rep)), jnp.tile(sb_g, (1, rep))


def kernel(x, meta_tokens, norm1_gain, w_in, q_norm_gain, k_norm_gain, lambda_q1, lambda_k1, lambda_q2,
           lambda_k2, subln_gain, conv_w, conv_b, conv_ln_gain, conv_ln_bias, w_out, norm2_gain, w_up, w_down):
    batch, seq, _ = x.shape
    assert norm1_gain.shape[0] == 1, "single-layer block"
    x2d = x.reshape(batch * seq, D_MODEL)

    g1 = norm1_gain[0].reshape(1, D_MODEL)
    g2 = norm2_gain[0].reshape(1, D_MODEL)
    win = w_in[0].astype(BF16)
    wo = w_out[0].astype(BF16)
    wu = w_up[0].astype(BF16)
    wd = w_down[0].astype(BF16)
    n_grp = D_ATTN // HEAD_DIM
    qg = jnp.tile(q_norm_gain[0], n_grp).reshape(1, D_ATTN)
    kg = jnp.tile(k_norm_gain[0], n_grp).reshape(1, D_ATTN)
    grp = jnp.arange(D_ATTN) // HEAD_DIM
    gmat = jnp.where(grp[:, None] == grp[None, :], 1.0 / HEAD_DIM, 0.0).astype(BF16)
    cos, sa, sb = _rope_tables(N_META + seq)

    tm = 512
    row = lambda p, w: p[0].reshape(1, w)
    km, vm, hm = _proj_meta_call(meta_tokens, g1, win, kg, gmat, cos[:N_META], sa[:N_META], sb[:N_META])
    q, k, v, conv = _proj_call(x2d, g1, win, qg, kg, gmat, cos[N_META:], sa[N_META:], sb[N_META:],
                               hm, conv_w[0], row(conv_b, D_CONV), row(conv_ln_gain, D_CONV),
                               row(conv_ln_bias, D_CONV), tm, seq // tm, 32)
    attn = _attn_call(q, k, v, km, vm.T, row(lambda_q1, HEAD_DIM), row(lambda_k1, HEAD_DIM),
                      row(lambda_q2, HEAD_DIM), row(lambda_k2, HEAD_DIM), subln_gain[0].reshape(V_HEAD_DIM, 1),
                      batch, seq, 512)
    out = _mlp_call(x2d, attn, conv, wo, g2, wu, wd, tm, 1024)
    return out.reshape(batch, seq, D_MODEL)
```

```python
import functools
import math

import jax
import jax.numpy as jnp
from jax import lax
from jax.experimental import pallas as pl
from jax.experimental.pallas import tpu as pltpu

D_MODEL = 1024
N_META = 16
D_ATTN = D_MODEL // 2
D_CONV = D_MODEL - D_ATTN
ATTN_HEADS = 4
HEAD_DIM = D_ATTN // (2 * ATTN_HEADS)
V_HEAD_DIM = 2 * HEAD_DIM
ROPE_DIM = HEAD_DIM // 4
ROPE_THETA = 500000.0
CONV_WIDTH = 31
D_FF = 4 * D_MODEL
EPS = 1e-6
LAMBDA_INIT = 0.8 - 0.6 * math.exp(-0.3 * 0)
D_IN = 3 * D_ATTN + 2 * D_CONV
LOG2_E = math.log2(math.e)

LANES = 128
CONV_HALO = 32
VMEM_LIMIT = 56 * 1024 * 1024

F32 = jnp.float32
BF16 = jnp.bfloat16


def _const_spec(shape):
    return pl.BlockSpec(shape, lambda *_: (0,) * len(shape), pipeline_mode=pl.Buffered(1))


def _head_norm_rope(t, gain, gmat, cos, sa, sb):
    msq = jnp.dot((t * t).astype(BF16), gmat, preferred_element_type=F32)
    tn = t * lax.rsqrt(msq + EPS) * gain
    cols = []
    for c in range(t.shape[1] // LANES):
        tc = tn[:, c * LANES:(c + 1) * LANES]
        cols.append(tc * cos + pltpu.roll(tc, LANES - ROPE_DIM // 2, 1) * sa + pltpu.roll(tc, ROPE_DIM // 2, 1) * sb)
    return jnp.concatenate(cols, axis=1)


def _normed_input(x_ref, g1_ref):
    x = x_ref[...]
    ms = jnp.mean(x * x, axis=-1, keepdims=True)
    return (x * lax.rsqrt(ms + EPS) * g1_ref[...]).astype(BF16)


def _proj_meta_kernel(x_ref, g1_ref, win_ref, kg_ref, gmat_ref, cos_ref, sa_ref, sb_ref, k_ref, v_ref, hg_ref):
    xn = _normed_input(x_ref, g1_ref)
    proj = lambda lo, width: jnp.dot(xn, win_ref[:, lo:lo + width], preferred_element_type=F32)
    k = _head_norm_rope(proj(D_ATTN, D_ATTN), kg_ref[...], gmat_ref[...], cos_ref[...], sa_ref[...], sb_ref[...])
    k_ref[...] = k.astype(BF16)
    v_ref[...] = proj(2 * D_ATTN, D_ATTN).astype(BF16)
    hg_ref[...] = proj(3 * D_ATTN, D_CONV) * jax.nn.sigmoid(proj(3 * D_ATTN + D_CONV, D_CONV))


def _proj_meta_call(meta, g1, win, kg, gmat, cos, sa, sb):
    full = lambda a: pl.BlockSpec(a.shape, lambda: (0,) * a.ndim)
    args = (meta, g1, win, kg, gmat, cos, sa, sb)
    return pl.pallas_call(
        _proj_meta_kernel,
        in_specs=[full(a) for a in args],
        out_specs=[pl.BlockSpec((N_META, w), lambda: (0, 0)) for w in (D_ATTN, D_ATTN, D_CONV)],
        out_shape=[jax.ShapeDtypeStruct((N_META, D_ATTN), BF16), jax.ShapeDtypeStruct((N_META, D_ATTN), BF16),
                   jax.ShapeDtypeStruct((N_META, D_CONV), F32)],
        compiler_params=pltpu.CompilerParams(vmem_limit_bytes=VMEM_LIMIT),
        name="proj_meta",
    )(*args)


def _proj_kernel(x_ref, g1_ref, win_ref, qg_ref, kg_ref, gmat_ref, cos_ref, sa_ref, sb_ref,
                 hm_ref, cw_ref, cb_ref, lg_ref, lb_ref,
                 q_ref, k_ref, v_ref, c_ref, hwin_ref, sh_ref, wb_ref, *, tm, tiles_per_batch, rc):
    j = pl.program_id(0) % tiles_per_batch
    xn = _normed_input(x_ref, g1_ref)
    proj = lambda lo, width: jnp.dot(xn, win_ref[:, lo:lo + width], preferred_element_type=F32)

    hg = proj(3 * D_ATTN, D_CONV) * jax.nn.sigmoid(proj(3 * D_ATTN + D_CONV, D_CONV))

    @pl.when(j == 0)
    def _():
        hwin_ref[0:CONV_HALO - N_META, :] = jnp.zeros((CONV_HALO - N_META, D_CONV), F32)
        hwin_ref[CONV_HALO - N_META:CONV_HALO, :] = hm_ref[...]

    @pl.when(j > 0)
    def _():
        hwin_ref[0:CONV_HALO, :] = hwin_ref[tm:tm + CONV_HALO, :]

    hwin_ref[CONV_HALO:, :] = hg
    n_sh = tm + CONV_HALO - 8
    for r in range(1, 8):
        sh_ref[r - 1, :, :] = hwin_ref[r:r + n_sh, :]

    rows_of = lambda v8: jnp.concatenate([v8] * (rc // 8), axis=0)
    for tap in range(CONV_WIDTH):
        wb_ref[tap] = jnp.broadcast_to(cw_ref[tap:tap + 1, :], (8, D_CONV))
    bias, ln_g, ln_b = (rows_of(jnp.broadcast_to(r[...], (8, D_CONV))) for r in (cb_ref, lg_ref, lb_ref))

    first = CONV_HALO - (CONV_WIDTH - 1)
    for c in range(tm // rc):
        r0 = c * rc
        acc = bias
        for tap in range(CONV_WIDTH):
            off = first + tap
            lo = r0 + (off // 8) * 8
            src = hwin_ref[lo:lo + rc, :] if off % 8 == 0 else sh_ref[off % 8 - 1, lo:lo + rc, :]
            acc = acc + src * rows_of(wb_ref[tap])
        mu = jnp.mean(acc, axis=-1, keepdims=True)
        cen = acc - mu
        var = jnp.mean(cen * cen, axis=-1, keepdims=True)
        y = cen * lax.rsqrt(var + EPS) * ln_g + ln_b
        c_ref[r0:r0 + rc, :] = (y * jax.nn.sigmoid(y)).astype(c_ref.dtype)

    gmat = gmat_ref[...]
    cos, sa, sb = cos_ref[...], sa_ref[...], sb_ref[...]
    q = _head_norm_rope(proj(0, D_ATTN), qg_ref[...] * (HEAD_DIM ** -0.5 * LOG2_E), gmat, cos, sa, sb)
    q_ref[...] = q.astype(BF16)
    k = _head_norm_rope(proj(D_ATTN, D_ATTN), kg_ref[...], gmat, cos, sa, sb)
    k_ref[...] = k.astype(BF16)
    v_ref[...] = proj(2 * D_ATTN, D_ATTN).astype(BF16)


def _proj_call(x2d, g1, win, qg, kg, gmat, cos, sa, sb, hm, cw, cb, lg, lb, tm, tiles_per_batch, rc):
    rows = x2d.shape[0]
    row_spec = lambda w: pl.BlockSpec((tm, w), lambda i: (i, 0))
    tab_spec = pl.BlockSpec((tm, LANES), lambda i: (i % tiles_per_batch, 0))
    vec = lambda w: _const_spec((1, w))
    return pl.pallas_call(
        functools.partial(_proj_kernel, tm=tm, tiles_per_batch=tiles_per_batch, rc=rc),
        grid=(rows // tm,),
        in_specs=[row_spec(D_MODEL), vec(D_MODEL), _const_spec((D_MODEL, D_IN)),
                  vec(D_ATTN), vec(D_ATTN), _const_spec((D_ATTN, D_ATTN)),
                  tab_spec, tab_spec, tab_spec,
                  _const_spec((N_META, D_CONV)), _const_spec((CONV_WIDTH, D_CONV)), vec(D_CONV), vec(D_CONV), vec(D_CONV)],
        out_specs=[row_spec(D_ATTN), row_spec(D_ATTN), row_spec(D_ATTN), row_spec(D_CONV)],
        out_shape=[jax.ShapeDtypeStruct((rows, D_ATTN), BF16), jax.ShapeDtypeStruct((rows, D_ATTN), BF16),
                   jax.ShapeDtypeStruct((rows, D_ATTN), BF16), jax.ShapeDtypeStruct((rows, D_CONV), BF16)],
        scratch_shapes=[pltpu.VMEM((tm + CONV_HALO, D_CONV), F32),
                        pltpu.VMEM((7, tm + CONV_HALO - 8, D_CONV), F32),
                        pltpu.VMEM((CONV_WIDTH, 8, D_CONV), F32)],
        compiler_params=pltpu.CompilerParams(dimension_semantics=("arbitrary",), vmem_limit_bytes=VMEM_LIMIT),
        name="proj_conv",
    )(x2d, g1, win, qg, kg, gmat, cos, sa, sb, hm, cw, cb, lg, lb)


KX = 2 * LANES
VX = V_HEAD_DIM + 16
OVERFLOW_GUARD = 2.0 ** 100
TILES_PER_TRIP = 4
SETUP_UNROLL = 4


def _split3(r):
    hi = r.astype(BF16).astype(F32)
    mid = (r - hi).astype(BF16).astype(F32)
    lo = (r - hi - mid).astype(BF16).astype(F32)
    return hi, mid, lo


def _attn_kernel(q_ref, k_ref, v_ref, km_ref, vmt_ref, lq1_ref, lk1_ref, lq2_ref, lk2_ref, sg_ref,
                 o_ref, kx_ref, vtx_ref, q2x_ref, acc_ref, p_ref, lag_ref, mx_ref, *, t, seq):
    nq = seq // t
    n_tiles = nq * (nq + 1) // 2
    assert nq % SETUP_UNROLL == 0

    def block_rows(i):
        start = i * t
        return pl.ds(start if isinstance(start, int) else pl.multiple_of(start, t), t)

    ones_col = (lax.broadcasted_iota(jnp.int32, (t, LANES), 1) < 3).astype(BF16)
    tail = (lax.broadcasted_iota(jnp.int32, (VX - V_HEAD_DIM, t), 0) == 0).astype(BF16)

    def ext(c, carry):
        rows = block_rows(c)
        kx_ref[rows, 0:LANES] = k_ref[rows, :]
        kx_ref[rows, LANES:KX] = ones_col
        vtx_ref[c, 0:V_HEAD_DIM, :] = v_ref[rows, :].astype(F32).T.astype(BF16)
        vtx_ref[c, V_HEAD_DIM:VX, :] = tail
        return carry
    lax.fori_loop(0, nq, ext, 0)

    row = lax.broadcasted_iota(jnp.int32, (t, 2 * t), 0)
    col = lax.broadcasted_iota(jnp.int32, (t, 2 * t), 1)
    lag_ref[...] = row - jnp.where(col >= t, col - t, col)

    ones16 = (lax.broadcasted_iota(jnp.int32, (N_META, LANES), 1) < 3).astype(BF16)
    kmx = jnp.concatenate([km_ref[...], ones16], axis=1)
    vtail = (lax.broadcasted_iota(jnp.int32, (VX - V_HEAD_DIM, N_META), 0) == 0).astype(BF16)
    vmtx = jnp.concatenate([vmt_ref[...], vtail], axis=0)

    def meta_scores(qi):
        return jnp.dot(kmx, q2x_ref[qi], preferred_element_type=F32)

    def set_reference(qi, ms, ref_max):
        hi, mid, lo = _split3(ref_max)
        pad = jnp.zeros((16 - 3, 2 * t), F32)
        q2x_ref[qi, LANES:LANES + 16, :] = (-jnp.concatenate([hi, mid, lo, pad], axis=0)).astype(BF16)
        pm = jnp.exp2(ms - (hi + mid + lo))
        acc_ref[qi] = jnp.dot(vmtx, pm.astype(BF16), preferred_element_type=F32)

    def setup(i, carry):
        for u in range(SETUP_UNROLL):
            qi = i * SETUP_UNROLL + u
            qt = q_ref[block_rows(qi), :].astype(F32).T
            feat = lax.broadcasted_iota(jnp.int32, qt.shape, 0)
            zero = jnp.zeros_like(qt)
            q2x_ref[qi, 0:LANES, 0:t] = jnp.where(feat < HEAD_DIM, qt, zero).astype(BF16)
            q2x_ref[qi, 0:LANES, t:2 * t] = jnp.where(feat >= HEAD_DIM, qt, zero).astype(BF16)
            q2x_ref[qi, LANES:KX, :] = jnp.zeros((KX - LANES, 2 * t), BF16)
            ms = meta_scores(qi)
            set_reference(qi, ms, jnp.max(ms, axis=0, keepdims=True))
        return carry
    lax.fori_loop(0, nq // SETUP_UNROLL, setup, 0)

    def scores(qi, g):
        st = jnp.dot(kx_ref[block_rows(g), :], q2x_ref[qi], preferred_element_type=F32)
        return jnp.where(lag_ref[...] <= (qi - g) * t, st, -jnp.inf)

    def qk_exp(qi, g, slot):
        p_ref[slot] = jnp.exp2(scores(qi, g)).astype(BF16)

    def pv(qi, g, slot):
        acc_ref[qi] += jnp.dot(vtx_ref[g], p_ref[slot], preferred_element_type=F32)

    def nxt(qi, g):
        last = g == qi
        return jnp.where(last, qi + 1, qi), jnp.where(last, 0, g + 1)

    def attend():
        qk_exp(0, 0, 0)

        def step(tile, slot):
            succ = nxt(*tile)
            pv(*tile, slot)
            qk_exp(*succ, 1 - slot)
            return succ

        def body(i, tile):
            for u in range(TILES_PER_TRIP):
                tile = step(tile, u % 2)
            return tile
        trips = (n_tiles - 1) // TILES_PER_TRIP
        tile = lax.fori_loop(0, trips, body, (jnp.int32(0), jnp.int32(0)))
        for u in range(n_tiles - 1 - trips * TILES_PER_TRIP):
            tile = step(tile, u % 2)
        pv(*tile, (n_tiles - 1) % 2)

    attend()

    @pl.when(jnp.logical_not(jnp.sum(acc_ref[:, V_HEAD_DIM:V_HEAD_DIM + 1, :]) < OVERFLOW_GUARD))
    def _():
        def clear(qi, carry):
            q2x_ref[qi, LANES:LANES + 16, :] = jnp.zeros((16, 2 * t), BF16)
            mx_ref[qi] = jnp.max(meta_scores(qi), axis=0, keepdims=True)
            return carry
        lax.fori_loop(0, nq, clear, 0)

        def mx(n, tile):
            qi, g = tile
            mx_ref[qi] = jnp.maximum(mx_ref[qi], jnp.max(scores(qi, g), axis=0, keepdims=True))
            return nxt(qi, g)
        lax.fori_loop(0, n_tiles, mx, (jnp.int32(0), jnp.int32(0)))

        def redo(qi, carry):
            set_reference(qi, meta_scores(qi), mx_ref[qi])
            return carry
        lax.fori_loop(0, nq, redo, 0)
        attend()

    lam = (jnp.exp(jnp.sum(lq1_ref[...] * lk1_ref[...], axis=-1, keepdims=True))
           - jnp.exp(jnp.sum(lq2_ref[...] * lk2_ref[...], axis=-1, keepdims=True)) + LAMBDA_INIT)
    gain = sg_ref[...] * (1.0 - LAMBDA_INIT)

    def finalize(qi, carry):
        acc = acc_ref[qi, 0:V_HEAD_DIM, :]
        l = acc_ref[qi, V_HEAD_DIM:V_HEAD_DIM + 1, :]
        o = acc[:, 0:t] / l[:, 0:t] - lam * (acc[:, t:2 * t] / l[:, t:2 * t])
        o = o * lax.rsqrt(jnp.mean(o * o, axis=0, keepdims=True) + EPS) * gain
        o_ref[block_rows(qi), :] = o.T.astype(o_ref.dtype)
        return carry
    lax.fori_loop(0, nq, finalize, 0)


def _attn_call(q, k, v, km, vmt, lq1, lk1, lq2, lk2, sg, batch, seq, t):
    nq = seq // t
    seq_blk = pl.BlockSpec((seq, LANES), lambda b, h: (b, h))
    small = lambda w: pl.BlockSpec((1, w), lambda b, h: (0, 0))
    return pl.pallas_call(
        functools.partial(_attn_kernel, t=t, seq=seq),
        grid=(batch, ATTN_HEADS),
        in_specs=[seq_blk, seq_blk, seq_blk,
                  pl.BlockSpec((N_META, LANES), lambda b, h: (0, h)),
                  pl.BlockSpec((V_HEAD_DIM, N_META), lambda b, h: (h, 0)),
                  small(HEAD_DIM), small(HEAD_DIM), small(HEAD_DIM), small(HEAD_DIM),
                  pl.BlockSpec((V_HEAD_DIM, 1), lambda b, h: (0, 0))],
        out_specs=seq_blk,
        out_shape=jax.ShapeDtypeStruct((batch * seq, D_ATTN), BF16),
        scratch_shapes=[pltpu.VMEM((seq, KX), BF16),
                        pltpu.VMEM((nq, VX, t), BF16),
                        pltpu.VMEM((nq, KX, 2 * t), BF16),
                        pltpu.VMEM((nq, VX, 2 * t), F32),
                        pltpu.VMEM((2, t, 2 * t), BF16),
                        pltpu.VMEM((t, 2 * t), jnp.int32),
                        pltpu.VMEM((nq, 1, 2 * t), F32)],
        compiler_params=pltpu.CompilerParams(dimension_semantics=("arbitrary",) * 2, vmem_limit_bytes=VMEM_LIMIT),
        name="diff_attn",
    )(q, k, v, km, vmt, lq1, lk1, lq2, lk2, sg)


def _mlp_kernel(x_ref, a_ref, c_ref, wo_ref, g2_ref, wu_ref, wd_ref, o_ref, *, ff_chunk):
    mixed = (jnp.dot(a_ref[...], wo_ref[0:D_ATTN, :], preferred_element_type=F32)
             + jnp.dot(c_ref[...], wo_ref[D_ATTN:, :], preferred_element_type=F32))
    h = x_ref[...] + mixed
    ms = jnp.mean(h * h, axis=-1, keepdims=True)
    hn = (h * lax.rsqrt(ms + EPS) * g2_ref[...]).astype(BF16)

    def ff_part(c):
        up = jnp.dot(hn, wu_ref[:, c * ff_chunk:(c + 1) * ff_chunk], preferred_element_type=F32)
        up = jnp.maximum(up, 0.0)
        return jnp.dot((up * up).astype(BF16), wd_ref[c * ff_chunk:(c + 1) * ff_chunk, :],
                       preferred_element_type=F32)

    ff = ff_part(0)
    for c in range(1, D_FF // ff_chunk):
        ff = ff + ff_part(c)
    o_ref[...] = h + ff


def _mlp_call(x2d, attn, conv, wo, g2, wu, wd, tm, ff_chunk):
    rows = x2d.shape[0]
    row_spec = lambda w: pl.BlockSpec((tm, w), lambda i: (i, 0))
    return pl.pallas_call(
        functools.partial(_mlp_kernel, ff_chunk=ff_chunk),
        grid=(rows // tm,),
        in_specs=[row_spec(D_MODEL), row_spec(D_ATTN), row_spec(D_CONV),
                  _const_spec((D_MODEL, D_MODEL)), _const_spec((1, D_MODEL)),
                  _const_spec((D_MODEL, D_FF)), _const_spec((D_FF, D_MODEL))],
        out_specs=row_spec(D_MODEL),
        out_shape=jax.ShapeDtypeStruct((rows, D_MODEL), F32),
        compiler_params=pltpu.CompilerParams(dimension_semantics=("arbitrary",), vmem_limit_bytes=VMEM_LIMIT),
        name="outproj_mlp",
    )(x2d, attn, conv, wo, g2, wu, wd)


def _rope_tables(length):
    half = ROPE_DIM // 2
    inv_freq = ROPE_THETA ** (-jnp.arange(0, ROPE_DIM, 2, dtype=F32) / ROPE_DIM)
    ang = jnp.arange(length, dtype=F32)[:, None] * inv_freq[None, :]
    cos, sin = jnp.cos(ang), jnp.sin(ang)
    ones = jnp.ones((length, HEAD_DIM - ROPE_DIM), F32)
    zeros = jnp.zeros((length, HEAD_DIM - ROPE_DIM), F32)
    zh = jnp.zeros((length, half), F32)
    cos_g = jnp.concatenate([cos, cos, ones], axis=1)
    sa_g = jnp.concatenate([-sin, zh, zeros], axis=1)
    sb_g = jnp.concatenate([zh, sin, zeros], axis=1)
    rep = LANES // HEAD_DIM
    return jnp.tile(cos_g, (1, rep)), jnp.tile(sa_g, (1, rep)), jnp.tile(sb_g, (1, rep))


def kernel(x, meta_tokens, norm1_gain, w_in, q_norm_gain, k_norm_gain, lambda_q1, lambda_k1, lambda_q2,
           lambda_k2, subln_gain, conv_w, conv_b, conv_ln_gain, conv_ln_bias, w_out, norm2_gain, w_up, w_down):
    batch, seq, _ = x.shape
    assert norm1_gain.shape[0] == 1, "single-layer block"
    x2d = x.reshape(batch * seq, D_MODEL)

    g1 = norm1_gain[0].reshape(1, D_MODEL)
    g2 = norm2_gain[0].reshape(1, D_MODEL)
    win = w_in[0].astype(BF16)
    wo = w_out[0].astype(BF16)
    wu = w_up[0].astype(BF16)
    wd = w_down[0].astype(BF16)
    n_grp = D_ATTN // HEAD_DIM
    qg = jnp.tile(q_norm_gain[0], n_grp).reshape(1, D_ATTN)
    kg = jnp.tile(k_norm_gain[0], n_grp).reshape(1, D_ATTN)
    grp = jnp.arange(D_ATTN) // HEAD_DIM
    gmat = jnp.where(grp[:, None] == grp[None, :], 1.0 / HEAD_DIM, 0.0).astype(BF16)
    cos, sa, sb = _rope_tables(N_META + seq)

    tm = 512
    row = lambda p, w: p[0].reshape(1, w)
    km, vm, hm = _proj_meta_call(meta_tokens, g1, win, kg, gmat, cos[:N_META], sa[:N_META], sb[:N_META])
    q, k, v, conv = _proj_call(x2d, g1, win, qg, kg, gmat, cos[N_META:], sa[N_META:], sb[N_META:],
                               hm, conv_w[0], row(conv_b, D_CONV), row(conv_ln_gain, D_CONV),
                               row(conv_ln_bias, D_CONV), tm, seq // tm, 32)
    attn = _attn_call(q, k, v, km, vm.T, row(lambda_q1, HEAD_DIM), row(lambda_k1, HEAD_DIM),
                      row(lambda_q2, HEAD_DIM), row(lambda_k2, HEAD_DIM), subln_gain[0].reshape(V_HEAD_DIM, 1),
                      batch, seq, 512)
    out = _mlp_call(x2d, attn, conv, wo, g2, wu, wd, tm, 1024)
    return out.reshape(batch, seq, D_MODEL)
```
